```python
import math
import jax, jax.numpy as jnp
from jax import lax
import numpy as np

D_MODEL = 1024
BATCH = 8
SEQ = 2048
DEPTH = 4
DEC_BATCH = 128
DEC_SEQ = 1
PAST_LEN = 8192
PAGE_SIZE = 128

SSM_WIDTH = D_MODEL // 2
SSM_GROUP = 16
SSM_GROUPS = SSM_WIDTH // SSM_GROUP
SSM_STATE = 64
DT_MIN = 1e-3
DT_MAX = 1e-1
MLA_HEADS = 8
QK_NOPE = 64
QK_ROPE = 32
V_HEAD = 64
Q_RANK = 3 * D_MODEL // 8
KV_RANK = D_MODEL // 4
ROPE_THETA = 10000.0
Q_BLOCK = 128
MEM_TOKENS = 256
MEM_HEADS = 4
MEM_HEAD_DIM = D_MODEL // 8
N_GROUPS = 4
EXPERTS_PER_GROUP = 8
N_EXPERTS = N_GROUPS * EXPERTS_PER_GROUP
TOP_K = 2
D_EXPERT = D_MODEL // 2
DN_ALPHA = (2 * DEPTH) ** 0.25
DN_BETA = (8 * DEPTH) ** -0.25
LN_EPS = 1e-5
NEG = -1e30
F32 = jnp.float32
O_Q = SSM_WIDTH
O_KV = O_Q + Q_RANK
O_KR = O_KV + KV_RANK
O_GS = O_KR + QK_ROPE
O_GM = O_GS + D_MODEL
N_IN = O_GM + D_MODEL

kernel_name = "hybrid_s5_mla_hmoe_decoder_step"


def layer_norm(x, g, b):
    x32 = x.astype(F32)
    mu = x32.mean(-1, keepdims=True)
    var = jnp.square(x32 - mu).mean(-1, keepdims=True)
    return ((x32 - mu) * lax.rsqrt(var + LN_EPS) * g + b).astype(x.dtype)


def rms_norm(x, g):
    x32 = x.astype(F32)
    return (x32 * lax.rsqrt(jnp.square(x32).mean(-1, keepdims=True) + LN_EPS) * g).astype(x.dtype)


def post_norm(h, f, g, b):
    return layer_norm(DN_ALPHA * h + f, g, b)


def rotary_tables(pos):
    inv = 1.0 / (ROPE_THETA ** (jnp.arange(0, QK_ROPE, 2, dtype=F32) / QK_ROPE))
    ang = pos.astype(F32)[:, None] * inv[None, :]
    return jnp.cos(ang), jnp.sin(ang)


def rope(x, cos, sin):
    x32 = x.astype(F32)
    half = x.shape[-1] // 2
    x1, x2 = x32[..., :half], x32[..., half:]
    return jnp.concatenate([x1 * cos - x2 * sin, x2 * cos + x1 * sin], -1).astype(x.dtype)


def s5_discretize(lam_re, lam_im, log_dt, b_re, b_im):
    lr, li = lam_re.astype(F32), lam_im.astype(F32)
    dt = jnp.exp(log_dt.astype(F32))[:, None]
    mag = jnp.exp(lr * dt)
    ab_re, ab_im = mag * jnp.cos(li * dt), mag * jnp.sin(li * dt)
    nr, ni = ab_re - 1.0, ab_im
    den = lr * lr + li * li
    f_re = ((nr * lr + ni * li) / den)[..., None]
    f_im = ((ni * lr - nr * li) / den)[..., None]
    br, bi = b_re.astype(F32), b_im.astype(F32)
    return ab_re, ab_im, f_re * br - f_im * bi, f_re * bi + f_im * br


def _complex_affine_combine(e1, e2):
    a1r, a1i, b1r, b1i = e1
    a2r, a2i, b2r, b2i = e2
    return (a2r * a1r - a2i * a1i, a2r * a1i + a2i * a1r,
            a2r * b1r - a2i * b1i + b2r, a2r * b1i + a2i * b1r + b2i)


def s5_scan(u, disc, c_re, c_im, d_skip, s0):
    bsz, slen, _ = u.shape
    ab_re, ab_im, bb_re, bb_im = disc
    ug = u.reshape(bsz, slen, SSM_GROUPS, SSM_GROUP).astype(F32)
    bu_re = jnp.einsum('bsgc,gpc->bsgp', ug, bb_re)
    bu_im = jnp.einsum('bsgc,gpc->bsgp', ug, bb_im)
    a_re = jnp.broadcast_to(ab_re, bu_re.shape)
    a_im = jnp.broadcast_to(ab_im, bu_im.shape)
    pr, pi, sr, si = lax.associative_scan(_complex_affine_combine, (a_re, a_im, bu_re, bu_im), axis=1)
    if s0 is not None:
        s0r = s0[0].astype(F32)[:, None]
        s0i = s0[1].astype(F32)[:, None]
        sr, si = sr + pr * s0r - pi * s0i, si + pr * s0i + pi * s0r
    y = (jnp.einsum('gcp,bsgp->bsgc', c_re.astype(F32), sr)
         - jnp.einsum('gcp,bsgp->bsgc', c_im.astype(F32), si)
         + d_skip.astype(F32).reshape(SSM_GROUPS, SSM_GROUP) * ug)
    return y.reshape(bsz, slen, SSM_WIDTH).astype(u.dtype), sr[:, -1].astype(u.dtype), si[:, -1].astype(u.dtype)


def mla_prompt_attn(q_nope, q_rope, ckv, k_rope, w_uk, w_uv):
    bsz, slen = q_nope.shape[:2]
    k_nope = jnp.einsum('bsc,che->bshe', ckv, w_uk)
    v = jnp.einsum('bsc,che->bshe', ckv, w_uv)
    nb = slen // Q_BLOCK
    qn = jnp.moveaxis(q_nope.reshape(bsz, nb, Q_BLOCK, MLA_HEADS, QK_NOPE), 1, 0)
    qr = jnp.moveaxis(q_rope.reshape(bsz, nb, Q_BLOCK, MLA_HEADS, QK_ROPE), 1, 0)
    starts = jnp.arange(nb) * Q_BLOCK
    scale = 1.0 / math.sqrt(QK_NOPE + QK_ROPE)
    k_pos = jnp.arange(slen)

    def block(args):
        qn_b, qr_b, start = args
        s = (jnp.einsum('bqhe,bkhe->bhqk', qn_b, k_nope)
             + jnp.einsum('bqhr,bkr->bhqk', qr_b, k_rope)).astype(F32) * scale
        causal = (start + jnp.arange(Q_BLOCK))[:, None] >= k_pos[None, :]
        p = jax.nn.softmax(jnp.where(causal, s, NEG), axis=-1)
        return jnp.einsum('bhqk,bkhe->bqhe', p.astype(v.dtype), v)

    o = lax.map(block, (qn, qr, starts))
    return jnp.moveaxis(o, 0, 1).reshape(bsz, slen, MLA_HEADS, V_HEAD)


def mla_sample_attn(q_nope, q_rope, ckv_new, kr_new, ckv_past, kr_past, w_uk, w_uv):
    scale = 1.0 / math.sqrt(QK_NOPE + QK_ROPE)
    q_lat = jnp.einsum('bqhe,che->bqhc', q_nope, w_uk)
    s_past = (jnp.einsum('bqhc,bkc->bhqk', q_lat, ckv_past)
              + jnp.einsum('bqhr,bkr->bhqk', q_rope, kr_past)).astype(F32) * scale
    s_new = (jnp.einsum('bqhc,bkc->bhqk', q_lat, ckv_new)
             + jnp.einsum('bqhr,bkr->bhqk', q_rope, kr_new)).astype(F32) * scale
    sq = q_nope.shape[1]
    causal = jnp.arange(sq)[:, None] >= jnp.arange(sq)[None, :]
    s = jnp.concatenate([s_past, jnp.where(causal, s_new, NEG)], axis=-1)
    p = jax.nn.softmax(s, axis=-1).astype(ckv_past.dtype)
    lp = ckv_past.shape[1]
    o_lat = (jnp.einsum('bhqk,bkc->bqhc', p[..., :lp], ckv_past)
             + jnp.einsum('bhqk,bkc->bqhc', p[..., lp:], ckv_new))
    return jnp.einsum('bqhc,che->bqhe', o_lat, w_uv)


def token_mixer(x, cos, sin, s0, past, w_in, w_out, disc, c_re, c_im, d_skip, w_glu, w_ps,
                q_norm_g, kv_norm_g, w_uq, w_uk, w_uv, w_pm):
    z = x @ w_in
    u, c_q, c_kv, k_r = z[..., :O_Q], z[..., O_Q:O_KV], z[..., O_KV:O_KR], z[..., O_KR:O_GS]
    g_s, g_m = z[..., O_GS:O_GM], z[..., O_GM:]
    y, s_re, s_im = s5_scan(u, disc, c_re, c_im, d_skip, s0)
    y = jax.nn.gelu(y)
    y = y * jax.nn.sigmoid(y @ w_glu)
    y_s = y @ w_ps
    q = jnp.einsum('bsr,rhe->bshe', rms_norm(c_q, q_norm_g), w_uq)
    q_nope = q[..., :QK_NOPE]
    q_rope = rope(q[..., QK_NOPE:], cos[:, None, :], sin[:, None, :])
    ckv = rms_norm(c_kv, kv_norm_g)
    k_rope = rope(k_r, cos, sin)
    if past is None:
        o = mla_prompt_attn(q_nope, q_rope, ckv, k_rope, w_uk, w_uv)
    else:
        o = mla_sample_attn(q_nope, q_rope, ckv, k_rope, past[0], past[1], w_uk, w_uv)
    y_m = o.reshape(o.shape[0], o.shape[1], MLA_HEADS * V_HEAD) @ w_pm
    merged = jax.nn.sigmoid(g_s) * y_s + jax.nn.sigmoid(g_m) * y_m
    return merged @ w_out, s_re, s_im, ckv, k_rope


def mem_kv(m, w_mk, w_mv):
    return jnp.einsum('bmd,dhe->bmhe', m, w_mk), jnp.einsum('bmd,dhe->bmhe', m, w_mv)


def mem_attn(x, k, v, w_mq, w_mo):
    q = jnp.einsum('bsd,dhe->bshe', x, w_mq)
    s = jnp.einsum('bshe,bmhe->bhsm', q, k).astype(F32) / math.sqrt(MEM_HEAD_DIM)
    p = jax.nn.softmax(s, axis=-1).astype(v.dtype)
    o = jnp.einsum('bhsm,bmhe->bshe', p, v)
    return jnp.einsum('bshe,hed->bsd', o, w_mo)


def hier_moe(x, w_r1, b_r1, w_r2, b_r2, w_eg, w_eu, w_ed):
    shp = x.shape
    t = x.reshape(-1, shp[-1])
    n_tok = t.shape[0]
    p1 = jax.nn.softmax((t @ w_r1 + b_r1).astype(F32), axis=-1)
    top_p1, g_idx = lax.top_k(p1, 1)
    sel = jax.nn.one_hot(g_idx[:, 0], N_GROUPS, dtype=F32)
    logit2 = (t @ w_r2 + b_r2).astype(F32).reshape(n_tok, N_GROUPS, EXPERTS_PER_GROUP)
    p2 = jax.nn.softmax((logit2 * sel[:, :, None]).sum(1), axis=-1)
    top_p2, e_idx = lax.top_k(p2, TOP_K)
    w = top_p1 * top_p2 / top_p2.sum(-1, keepdims=True)
    e_glob = g_idx * EXPERTS_PER_GROUP + e_idx
    gates = (jax.nn.one_hot(e_glob, N_EXPERTS, dtype=F32) * w[..., None]).sum(1)
    y = jnp.zeros_like(t)
    for g in range(N_GROUPS):
        sl = slice(g * EXPERTS_PER_GROUP, (g + 1) * EXPERTS_PER_GROUP)
        h = jax.nn.silu(jnp.einsum('td,edf->tef', t, w_eg[sl])) * jnp.einsum('td,edf->tef', t, w_eu[sl])
        h = h * gates[:, sl, None].astype(h.dtype)
        y = y + jnp.einsum('tef,efd->td', h, w_ed[sl])
    return y.reshape(shp)


def setup_inputs(seed: int = 0) -> dict:
    key = jax.random.key(seed)
    ks = iter(jax.random.split(key, 64))

    def nrm(shape, scale):
        return jax.random.normal(next(ks), shape, F32) * scale

    n_pages = PAST_LEN // PAGE_SIZE
    used = DEC_BATCH * n_pages
    n_pool = used + (used + 3) // 4
    L, D = DEPTH, D_MODEL
    inp = {}
    inp['x_prompt'] = nrm((BATCH, SEQ, D), 1.0)
    inp['x_sample'] = nrm((DEC_BATCH, DEC_SEQ, D), 1.0)
    inp['cache_ckv'] = nrm((L, n_pool, PAGE_SIZE, KV_RANK), 1.0)
    inp['cache_krope'] = nrm((L, n_pool, PAGE_SIZE, QK_ROPE), 1.0)
    inp['cache_mem_k'] = nrm((L, DEC_BATCH, MEM_TOKENS, MEM_HEADS, MEM_HEAD_DIM), 1.0)
    inp['cache_mem_v'] = nrm((L, DEC_BATCH, MEM_TOKENS, MEM_HEADS, MEM_HEAD_DIM), DN_BETA)
    inp['state_ssm_re'] = nrm((L, DEC_BATCH, SSM_GROUPS, SSM_STATE), 0.3)
    inp['state_ssm_im'] = nrm((L, DEC_BATCH, SSM_GROUPS, SSM_STATE), 0.3)
    inp['page_table'] = jax.random.permutation(next(ks), n_pool)[:used].reshape(DEC_BATCH, n_pages).astype(jnp.int32)
    inp['mem_prompt'] = nrm((BATCH, MEM_TOKENS, D), 1.0)
    inp['mem_ln_g'] = 1.0 + nrm((D,), 0.02)
    inp['mem_ln_b'] = nrm((D,), 0.02)
    inp['w_in'] = nrm((L, D, N_IN), D ** -0.5)
    inp['w_uq'] = nrm((L, Q_RANK, MLA_HEADS, QK_NOPE + QK_ROPE), Q_RANK ** -0.5)
    inp['q_norm_g'] = 1.0 + nrm((L, Q_RANK), 0.02)
    inp['kv_norm_g'] = 1.0 + nrm((L, KV_RANK), 0.02)
    inp['w_uk'] = nrm((L, KV_RANK, MLA_HEADS, QK_NOPE), KV_RANK ** -0.5)
    inp['w_uv'] = nrm((L, KV_RANK, MLA_HEADS, V_HEAD), DN_BETA * KV_RANK ** -0.5)
    inp['w_pm'] = nrm((L, MLA_HEADS * V_HEAD, D), (MLA_HEADS * V_HEAD) ** -0.5)
    inp['lam_re'] = -0.5 * jnp.exp(nrm((L, SSM_GROUPS, SSM_STATE), 0.02))
    inp['lam_im'] = jnp.broadcast_to(math.pi * jnp.arange(SSM_STATE, dtype=F32), (L, SSM_GROUPS, SSM_STATE))
    inp['log_dt'] = jax.random.uniform(next(ks), (L, SSM_GROUPS), F32, math.log(DT_MIN), math.log(DT_MAX))
    inp['b_re'] = nrm((L, SSM_GROUPS, SSM_STATE, SSM_GROUP), (2 * SSM_GROUP) ** -0.5)
    inp['b_im'] = nrm((L, SSM_GROUPS, SSM_STATE, SSM_GROUP), (2 * SSM_GROUP) ** -0.5)
    inp['c_re'] = nrm((L, SSM_GROUPS, SSM_GROUP, SSM_STATE), (2 * SSM_STATE) ** -0.5)
    inp['c_im'] = nrm((L, SSM_GROUPS, SSM_GROUP, SSM_STATE), (2 * SSM_STATE) ** -0.5)
    inp['d_skip'] = nrm((L, SSM_WIDTH), 1.0)
    inp['w_glu'] = nrm((L, SSM_WIDTH, SSM_WIDTH), SSM_WIDTH ** -0.5)
    inp['w_ps'] = nrm((L, SSM_WIDTH, D), SSM_WIDTH ** -0.5)
    inp['w_out'] = nrm((L, D, D), DN_BETA * D ** -0.5)
    inp['ln1_g'] = 1.0 + nrm((L, D), 0.02)
    inp['ln1_b'] = nrm((L, D), 0.02)
    inp['w_mq'] = nrm((L, D, MEM_HEADS, MEM_HEAD_DIM), D ** -0.5)
    inp['w_mk'] = nrm((L, D, MEM_HEADS, MEM_HEAD_DIM), D ** -0.5)
    inp['w_mv'] = nrm((L, D, MEM_HEADS, MEM_HEAD_DIM), DN_BETA * D ** -0.5)
    inp['w_mo'] = nrm((L, MEM_HEADS, MEM_HEAD_DIM, D), DN_BETA * (MEM_HEADS * MEM_HEAD_DIM) ** -0.5)
    inp['ln2_g'] = 1.0 + nrm((L, D), 0.02)
    inp['ln2_b'] = nrm((L, D), 0.02)
    inp['w_r1'] = nrm((L, D, N_GROUPS), D ** -0.5)
    inp['b_r1'] = nrm((L, N_GROUPS), 0.01)
    inp['w_r2'] = nrm((L, D, N_EXPERTS), D ** -0.5)
    inp['b_r2'] = nrm((L, N_EXPERTS), 0.01)
    inp['w_eg'] = nrm((L, N_EXPERTS, D, D_EXPERT), D ** -0.5)
    inp['w_eu'] = nrm((L, N_EXPERTS, D, D_EXPERT), D ** -0.5)
    inp['w_ed'] = nrm((L, N_EXPERTS, D_EXPERT, D), DN_BETA * D_EXPERT ** -0.5)
    inp['ln3_g'] = 1.0 + nrm((L, D), 0.02)
    inp['ln3_b'] = nrm((L, D), 0.02)
    return inp


def reference(x_prompt, x_sample, cache_ckv, cache_krope, cache_mem_k, cache_mem_v,
              state_ssm_re, state_ssm_im, page_table, mem_prompt,
              mem_ln_g, mem_ln_b, w_in, w_uq, q_norm_g, kv_norm_g, w_uk, w_uv, w_pm,
              lam_re, lam_im, log_dt, b_re, b_im, c_re, c_im, d_skip, w_glu, w_ps, w_out,
              ln1_g, ln1_b, w_mq, w_mk, w_mv, w_mo, ln2_g, ln2_b,
              w_r1, b_r1, w_r2, b_r2, w_eg, w_eu, w_ed, ln3_g, ln3_b):
    n_dec = x_sample.shape[0]
    past_len = page_table.shape[1] * PAGE_SIZE
    cos_p, sin_p = rotary_tables(jnp.arange(x_prompt.shape[1]))
    cos_s, sin_s = rotary_tables(past_len + jnp.arange(x_sample.shape[1]))
    m_p = layer_norm(mem_prompt, mem_ln_g, mem_ln_b)

    hp, hs = x_prompt, x_sample
    p_sre, p_sim, p_ckv, p_kr, p_mk, p_mv = [], [], [], [], [], []
    s_sre, s_sim, s_ckv, s_kr = [], [], [], []
    for l in range(DEPTH):
        disc = s5_discretize(lam_re[l], lam_im[l], log_dt[l], b_re[l], b_im[l])
        out_p, sr_p, si_p, ckv_p, kr_p = token_mixer(
            hp, cos_p, sin_p, None, None, w_in[l], w_out[l], disc, c_re[l], c_im[l], d_skip[l],
            w_glu[l], w_ps[l], q_norm_g[l], kv_norm_g[l], w_uq[l], w_uk[l], w_uv[l], w_pm[l])
        hp = post_norm(hp, out_p, ln1_g[l], ln1_b[l])
        mk_p, mv_p = mem_kv(m_p, w_mk[l], w_mv[l])
        hp = post_norm(hp, mem_attn(hp, mk_p, mv_p, w_mq[l], w_mo[l]), ln2_g[l], ln2_b[l])
        hp = post_norm(hp, hier_moe(hp, w_r1[l], b_r1[l], w_r2[l], b_r2[l], w_eg[l], w_eu[l], w_ed[l]),
                       ln3_g[l], ln3_b[l])
        p_sre.append(sr_p); p_sim.append(si_p); p_ckv.append(ckv_p); p_kr.append(kr_p)
        p_mk.append(mk_p); p_mv.append(mv_p)
        ckv_past = cache_ckv[l, page_table].reshape(n_dec, past_len, KV_RANK)
        kr_past = cache_krope[l, page_table].reshape(n_dec, past_len, QK_ROPE)
        out_s, sr_s, si_s, ckv_s, kr_s = token_mixer(
            hs, cos_s, sin_s, (state_ssm_re[l], state_ssm_im[l]), (ckv_past, kr_past),
            w_in[l], w_out[l], disc, c_re[l], c_im[l], d_skip[l],
            w_glu[l], w_ps[l], q_norm_g[l], kv_norm_g[l], w_uq[l], w_uk[l], w_uv[l], w_pm[l])
        hs = post_norm(hs, out_s, ln1_g[l], ln1_b[l])
        hs = post_norm(hs, mem_attn(hs, cache_mem_k[l], cache_mem_v[l], w_mq[l], w_mo[l]), ln2_g[l], ln2_b[l])
        hs = post_norm(hs, hier_moe(hs, w_r1[l], b_r1[l], w_r2[l], b_r2[l], w_eg[l], w_eu[l], w_ed[l]),
                       ln3_g[l], ln3_b[l])
        s_sre.append(sr_s); s_sim.append(si_s); s_ckv.append(ckv_s); s_kr.append(kr_s)

    return (hp, hs,
            jnp.stack(p_sre), jnp.stack(p_sim), jnp.stack(p_ckv), jnp.stack(p_kr),
            jnp.stack(p_mk), jnp.stack(p_mv),
            jnp.stack(s_sre), jnp.stack(s_sim), jnp.stack(s_ckv), jnp.stack(s_kr))
```

```python
import functools
import math

import jax
import jax.numpy as jnp
from jax import lax
from jax.experimental import pallas as pl
from jax.experimental.pallas import tpu as pltpu

F32 = jnp.float32
BF16 = jnp.bfloat16
I32 = jnp.int32

D_MODEL = 1024
BATCH = 8
SEQ = 2048
DEPTH = 4
DEC_BATCH = 128
PAGE_SIZE = 128
SSM_WIDTH = 512
SSM_GROUP = 16
SSM_GROUPS = 32
SSM_STATE = 64
SSM_LANES = SSM_GROUPS * SSM_STATE
MLA_HEADS = 8
QK_NOPE = 64
QK_ROPE = 32
V_HEAD = 64
Q_RANK = 384
KV_RANK = 256
ROPE_THETA = 10000.0
MEM_TOKENS = 256
MEM_HEADS = 4
MEM_HEAD_DIM = 128
N_GROUPS = 4
EXPERTS_PER_GROUP = 8
N_EXPERTS = 32
D_EXPERT = 512
DN_ALPHA = (2 * DEPTH) ** 0.25
LN_EPS = 1e-5
NEG = -1e30
O_Q = SSM_WIDTH
O_KV = O_Q + Q_RANK
O_KR = O_KV + KV_RANK
O_GS = O_KR + QK_ROPE
O_GM = O_GS + D_MODEL

LANES = 128
SUBLANES = 8
VMEM_LIMIT_BYTES = 56 * 1024 * 1024

T_P = BATCH * SEQ
T_S = DEC_BATCH
T_ALL = T_P + T_S
TM = 512
N_ST = SEQ // TM
S_BASE = T_P // T_S
HEAD_PAD = LANES
QK_W = MLA_HEADS * HEAD_PAD
S5_LT = 64
S5_CB = 1024
MOE_TM = 256
MOE_TT = 384
N_TT = T_ALL // MOE_TT
N_PAIRS = 2 * T_ALL
MOE_NT = (N_PAIRS + N_EXPERTS * (MOE_TM - 1) + MOE_TM - 1) // MOE_TM
MOE_ROWS = MOE_NT * MOE_TM
POS_W = 1024
MEMS_G = 8


def _cparams(*sem):
    return pltpu.CompilerParams(dimension_semantics=sem, vmem_limit_bytes=VMEM_LIMIT_BYTES)


def _dot(a, b):
    return jnp.dot(a, b, preferred_element_type=F32)


def _dot_nt(a, b):
    return lax.dot_general(a, b, (((1,), (1,)), ((), ())), preferred_element_type=F32)


def _sigmoid(x):
    return 1.0 / (1.0 + jnp.exp(-x))


def _layer_norm(x, g, b):
    mu = jnp.mean(x, axis=-1, keepdims=True)
    xc = x - mu
    var = jnp.mean(xc * xc, axis=-1, keepdims=True)
    return xc * lax.rsqrt(var + LN_EPS) * g + b


def _rms_norm(x, g):
    return x * lax.rsqrt(jnp.mean(x * x, axis=-1, keepdims=True) + LN_EPS) * g


def _mm_kernel(x_ref, w_ref, o_ref):
    o_ref[...] = _dot(x_ref[...].astype(BF16), w_ref[...]).astype(o_ref.dtype)


def _mm(x, w, tm, out_dtype=F32):
    m, k = x.shape
    n = w.shape[1]
    return pl.pallas_call(
        _mm_kernel,
        grid=(m // tm,),
        in_specs=[pl.BlockSpec((tm, k), lambda i: (i, 0)), pl.BlockSpec((k, n), lambda i: (0, 0))],
        out_specs=pl.BlockSpec((tm, n), lambda i: (i, 0)),
        out_shape=jax.ShapeDtypeStruct((m, n), out_dtype),
        compiler_params=_cparams("parallel"),
        name="mm",
    )(x, w)


def _ln_kernel(x_ref, g_ref, b_ref, o_ref):
    o_ref[...] = _layer_norm(x_ref[...], g_ref[...], b_ref[...])


def _ln_rows(x, g, b, tm):
    m, d = x.shape
    return pl.pallas_call(
        _ln_kernel,
        grid=(m // tm,),
        in_specs=[pl.BlockSpec((tm, d), lambda i: (i, 0)),
                  pl.BlockSpec((1, d), lambda i: (0, 0)), pl.BlockSpec((1, d), lambda i: (0, 0))],
        out_specs=pl.BlockSpec((tm, d), lambda i: (i, 0)),
        out_shape=jax.ShapeDtypeStruct((m, d), F32),
        compiler_params=_cparams("parallel"),
        name="mem_ln",
    )(x, g.reshape(1, d), b.reshape(1, d))


def _disc_kernel(lr_ref, li_ref, ldt_ref, br_ref, bi_ref, ar_ref, ai_ref, bbr_ref, bbi_ref):
    lr, li = lr_ref[...], li_ref[...]
    dt = jnp.exp(ldt_ref[...])
    mag = jnp.exp(lr * dt)
    ab_re, ab_im = mag * jnp.cos(li * dt), mag * jnp.sin(li * dt)
    nr, ni = ab_re - 1.0, ab_im
    den = lr * lr + li * li
    f_re = (nr * lr + ni * li) / den
    f_im = (ni * lr - nr * li) / den
    br, bi = br_ref[...], bi_ref[...]
    ar_ref[...] = ab_re
    ai_ref[...] = ab_im
    bbr_ref[...] = f_re * br - f_im * bi
    bbi_ref[...] = f_re * bi + f_im * br


def _discretize(lam_re, lam_im, log_dt, b_re, b_im):
    rows = DEPTH * SSM_GROUPS
    cols = SSM_STATE * SSM_GROUP
    rep = lambda a: jnp.repeat(a.reshape(rows, SSM_STATE), SSM_GROUP, axis=1)
    ldt = jnp.broadcast_to(log_dt.reshape(rows, 1), (rows, cols))
    spec = pl.BlockSpec((rows, cols), lambda: (0, 0))
    sds = jax.ShapeDtypeStruct((rows, cols), F32)
    ar, ai, bbr, bbi = pl.pallas_call(
        _disc_kernel, in_specs=[spec] * 5, out_specs=[spec] * 4, out_shape=[sds] * 4, name="s5_disc",
    )(rep(lam_re), rep(lam_im), ldt, b_re.reshape(rows, cols), b_im.reshape(rows, cols))
    ar = ar.reshape(DEPTH, SSM_GROUPS, SSM_STATE, SSM_GROUP)[..., 0]
    ai = ai.reshape(DEPTH, SSM_GROUPS, SSM_STATE, SSM_GROUP)[..., 0]
    shp = (DEPTH, SSM_GROUPS, SSM_STATE, SSM_GROUP)
    return ar, ai, bbr.reshape(shp), bbi.reshape(shp)


def _proj_in_kernel(*refs, prompt):
    if prompt:
        (x_ref, wm_ref, qg_ref, kvg_ref, wa_ref, wb_ref, wk_ref, wv_ref, caq_ref, sbq_ref, ck_ref, sk_ref,
         u_ref, q_ref, ckv_ref, kr_ref, k_ref, v_ref) = refs
    else:
        (x_ref, wm_ref, qg_ref, kvg_ref, wa_ref, wb_ref, wukt_ref, war_ref, wbr_ref, caq_ref, sbq_ref,
         ck_ref, sk_ref, cr_ref, sr_ref, u_ref, ckv_ref, kr_ref, qlat_ref, qrope_ref) = refs
    z = _dot(x_ref[...].astype(BF16), wm_ref[...])
    u_ref[...] = z[:, :O_Q]
    xn = _rms_norm(z[:, O_Q:O_KV], qg_ref[...]).astype(BF16)
    ckvn = _rms_norm(z[:, O_KV:O_KR], kvg_ref[...])
    ckv_ref[...] = ckvn
    o_rot = O_KR + HEAD_PAD
    krp = z[:, O_KR:o_rot] * ck_ref[...] + z[:, o_rot:o_rot + HEAD_PAD] * sk_ref[...]
    kr_ref[...] = krp
    caq = jnp.concatenate([caq_ref[...]] * MLA_HEADS, axis=1)
    sbq = jnp.concatenate([sbq_ref[...]] * MLA_HEADS, axis=1)
    q = _dot(xn, wa_ref[...]) * caq + _dot(xn, wb_ref[...]) * sbq
    if prompt:
        q_ref[...] = q.astype(BF16)
        cb = ckvn.astype(BF16)
        k = _dot(cb, wk_ref[...]) + jnp.concatenate([krp] * MLA_HEADS, axis=1)
        k_ref[...] = k.astype(BF16)
        v_ref[...] = _dot(cb, wv_ref[...]).astype(BF16)
    else:
        qb = q.astype(BF16)
        for h in range(MLA_HEADS):
            qlat_ref[:, h * KV_RANK:(h + 1) * KV_RANK] = _dot(qb[:, h * HEAD_PAD:(h + 1) * HEAD_PAD], wukt_ref[h])
        qrope_ref[...] = _dot(xn, war_ref[...]) * cr_ref[...] + _dot(xn, wbr_ref[...]) * sr_ref[...]


def _proj_in(hbuf, lw, tabs, *, prompt):
    n_main = lw["wmain"].shape[1]
    if prompt:
        tm, grid = TM, (BATCH, N_ST)
        row = lambda b, s: (b * N_ST + s, 0)
        tab = lambda b, s: (s, 0)
        u_shape, u_spec = (SEQ, BATCH * SSM_WIDTH), pl.BlockSpec((tm, SSM_WIDTH), lambda b, s: (s, b))
        n_rows = T_P
    else:
        tm, grid = T_S, (1, 1)
        row = lambda b, s: (0, 0)
        tab = lambda b, s: (0, 0)
        u_shape, u_spec = (T_S, SSM_WIDTH), pl.BlockSpec((tm, SSM_WIDTH), row)
        n_rows = T_S
    x_row = row if prompt else (lambda b, s: (S_BASE, 0))
    full = lambda shape: pl.BlockSpec(shape, lambda b, s: (0,) * len(shape))
    rows = lambda n: pl.BlockSpec((tm, n), row)
    tspec = lambda n: pl.BlockSpec((tm, n), tab)
    in_specs = [pl.BlockSpec((tm, D_MODEL), x_row), full((D_MODEL, n_main)), full((1, Q_RANK)), full((1, KV_RANK)),
                full((Q_RANK, QK_W)), full((Q_RANK, QK_W))]
    args = [hbuf, lw["wmain"], lw["q_norm_g"], lw["kv_norm_g"], lw["wa"], lw["wb"]]
    if prompt:
        in_specs += [full((KV_RANK, QK_W)), full((KV_RANK, QK_W))]
        args += [lw["wk_pad"], lw["wv_pad"]]
    else:
        n_r = MLA_HEADS * QK_ROPE
        in_specs += [full((MLA_HEADS, HEAD_PAD, KV_RANK)), full((Q_RANK, n_r)), full((Q_RANK, n_r))]
        args += [lw["wukt_pad"], lw["wa_r"], lw["wb_r"]]
    in_specs += [tspec(HEAD_PAD)] * 4
    args += [tabs["caq"], tabs["sbq"], tabs["ck"], tabs["sk"]]
    if not prompt:
        in_specs += [tspec(MLA_HEADS * QK_ROPE)] * 2
        args += [tabs["cr"], tabs["sr"]]
    sds = lambda n, dt=F32: jax.ShapeDtypeStruct((n_rows, n), dt)
    if prompt:
        out_specs = [u_spec, rows(QK_W), rows(KV_RANK), rows(HEAD_PAD), rows(QK_W), rows(QK_W)]
        out_shape = [jax.ShapeDtypeStruct(u_shape, F32), sds(QK_W, BF16), sds(KV_RANK), sds(HEAD_PAD),
                     sds(QK_W, BF16), sds(QK_W, BF16)]
    else:
        out_specs = [u_spec, rows(KV_RANK), rows(HEAD_PAD), rows(MLA_HEADS * KV_RANK), rows(MLA_HEADS * QK_ROPE)]
        out_shape = [jax.ShapeDtypeStruct(u_shape, F32), sds(KV_RANK), sds(HEAD_PAD),
                     sds(MLA_HEADS * KV_RANK), sds(MLA_HEADS * QK_ROPE)]
    return pl.pallas_call(
        functools.partial(_proj_in_kernel, prompt=prompt),
        grid=grid, in_specs=in_specs, out_specs=out_specs, out_shape=out_shape,
        compiler_params=_cparams("parallel", "parallel"),
        name="proj_in_p" if prompt else "proj_in_s",
    )(*args)


def _gelu_tanh(x):
    return 0.5 * x * (1.0 + jnp.tanh(math.sqrt(2.0 / math.pi) * (x + 0.044715 * (x * x * x))))


def _s5_kernel(u_ref, s0r_ref, s0i_ref, ar_ref, ai_ref, bm_ref, cm_ref, dsk_ref, wglu_ref,
               y_ref, sr_ref, si_ref, bre, bim, *, n_rows, n_steps):
    @pl.when(pl.program_id(0) == 0)
    def _():
        sr_ref[...] = s0r_ref[...]
        si_ref[...] = s0i_ref[...]

    u = u_ref[...]
    ub = u.astype(BF16)
    n_blk = SSM_WIDTH // LANES
    half = SSM_LANES // n_blk
    for j in range(n_blk):
        bu = _dot(ub[:, j * LANES:(j + 1) * LANES], bm_ref[j])
        bre[:, j * half:(j + 1) * half] = bu[:, :half]
        bim[:, j * half:(j + 1) * half] = bu[:, half:]

    for c in range(SSM_LANES // S5_CB):
        cs = slice(c * S5_CB, (c + 1) * S5_CB)
        a_re = jnp.broadcast_to(ar_ref[:, cs], (n_rows, S5_CB))
        a_im = jnp.broadcast_to(ai_ref[:, cs], (n_rows, S5_CB))

        def step(t, carry):
            s_re, s_im = carry
            rs = pl.ds(0, n_rows) if n_steps == 1 else pl.ds(pl.multiple_of(t * n_rows, n_rows), n_rows)
            n_re = a_re * s_re - a_im * s_im + bre[rs, cs]
            n_im = a_re * s_im + a_im * s_re + bim[rs, cs]
            bre[rs, cs] = n_re
            bim[rs, cs] = n_im
            return n_re, n_im

        carry = (sr_ref[:, cs], si_ref[:, cs])
        if n_steps == 1:
            carry = step(0, carry)
        else:
            carry = lax.fori_loop(0, n_steps, step, carry, unroll=8)
        sr_ref[:, cs] = carry[0]
        si_ref[:, cs] = carry[1]

    ys = []
    for j in range(n_blk):
        s_re = bre[:, j * half:(j + 1) * half].astype(BF16)
        s_im = bim[:, j * half:(j + 1) * half].astype(BF16)
        ys.append(_dot(s_re, cm_ref[j, :half, :]) + _dot(s_im, cm_ref[j, half:, :]))
    y = jnp.concatenate(ys, axis=1) + dsk_ref[...] * u
    y = _gelu_tanh(y)
    y = y * _sigmoid(_dot(y.astype(BF16), wglu_ref[...]))
    y_ref[...] = y.astype(BF16)


def _s5(u_rows, s0_re, s0_im, lw, *, n_rows, n_steps):
    total = u_rows.shape[0]
    rows_per = n_rows * n_steps
    n_blk = SSM_WIDTH // LANES
    half = SSM_LANES // n_blk
    full = lambda shape: pl.BlockSpec(shape, lambda c: (0,) * len(shape))
    return pl.pallas_call(
        functools.partial(_s5_kernel, n_rows=n_rows, n_steps=n_steps),
        grid=(total // rows_per,),
        in_specs=[pl.BlockSpec((rows_per, SSM_WIDTH), lambda c: (c, 0)),
                  full((n_rows, SSM_LANES)), full((n_rows, SSM_LANES)),
                  full((1, SSM_LANES)), full((1, SSM_LANES)),
                  full((n_blk, LANES, 2 * half)), full((n_blk, 2 * half, LANES)),
                  full((1, SSM_WIDTH)), full((SSM_WIDTH, SSM_WIDTH))],
        out_specs=[pl.BlockSpec((rows_per, SSM_WIDTH), lambda c: (c, 0)),
                   full((n_rows, SSM_LANES)), full((n_rows, SSM_LANES))],
        out_shape=[jax.ShapeDtypeStruct((total, SSM_WIDTH), BF16),
                   jax.ShapeDtypeStruct((n_rows, SSM_LANES), F32),
                   jax.ShapeDtypeStruct((n_rows, SSM_LANES), F32)],
        scratch_shapes=[pltpu.VMEM((rows_per, SSM_LANES), F32), pltpu.VMEM((rows_per, SSM_LANES), F32)],
        compiler_params=_cparams("arbitrary"),
        name="s5_p" if n_steps > 1 else "s5_s",
    )(u_rows, s0_re, s0_im, lw["a_re"], lw["a_im"], lw["bm"], lw["cm"], lw["d_skip"], lw["w_glu"])


def _flash_kernel(q_ref, k_ref, v_ref, o_ref, *, tq):
    qi = pl.program_id(1)
    row = lax.broadcasted_iota(I32, (tq, tq), 0)
    col = lax.broadcasted_iota(I32, (tq, tq), 1)
    causal = row >= col
    for hp in range(MLA_HEADS // 2):
        pair = jnp.zeros((tq, HEAD_PAD), F32)
        for hh in range(2):
            hs = slice((2 * hp + hh) * HEAD_PAD, (2 * hp + hh + 1) * HEAD_PAD)
            q = q_ref[:, hs]

            def step(ki, carry, masked):
                m, l, acc = carry
                ks = pl.ds(pl.multiple_of(ki * tq, tq), tq)
                s = _dot_nt(q, k_ref[ks, hs])
                if masked:
                    s = jnp.where(causal, s, NEG)
                m_new = jnp.maximum(m, jnp.max(s, axis=-1, keepdims=True))
                p = jnp.exp(s - m_new)
                alpha = jnp.exp(m - m_new)
                l = alpha * l + jnp.sum(p, axis=-1, keepdims=True)
                acc = alpha * acc + _dot(p.astype(BF16), v_ref[ks, hs])
                return m_new, l, acc

            init = (jnp.full((tq, 1), NEG, F32), jnp.zeros((tq, 1), F32), jnp.zeros((tq, HEAD_PAD), F32))
            carry = lax.fori_loop(0, qi, functools.partial(step, masked=False), init)
            m, l, acc = step(qi, carry, True)
            pair = pair + acc / l
        o_ref[:, hp * HEAD_PAD:(hp + 1) * HEAD_PAD] = pair.astype(BF16)


def _flash(q, k, v):
    tq = TM
    return pl.pallas_call(
        functools.partial(_flash_kernel, tq=tq),
        grid=(BATCH, N_ST),
        in_specs=[pl.BlockSpec((tq, QK_W), lambda b, i: (b * N_ST + i, 0)),
                  pl.BlockSpec((SEQ, QK_W), lambda b, i: (b, 0)),
                  pl.BlockSpec((SEQ, QK_W), lambda b, i: (b, 0))],
        out_specs=pl.BlockSpec((tq, MLA_HEADS * V_HEAD), lambda b, i: (b * N_ST + i, 0)),
        out_shape=jax.ShapeDtypeStruct((T_P, MLA_HEADS * V_HEAD), BF16),
        compiler_params=_cparams("parallel", "arbitrary"),
        name="flash_p",
    )(q, k, v)


def _page_copies(pt_ref, ckv_hbm, kr_hbm, buf_c, buf_k, sem, layer, seq, slot, n_pages):
    copies = []
    for j in range(n_pages):
        pg = pt_ref[seq * n_pages + j]
        rows = pl.ds(j * PAGE_SIZE, PAGE_SIZE)
        copies.append(pltpu.make_async_copy(ckv_hbm.at[layer, pg], buf_c.at[slot, rows], sem.at[0, slot]))
        copies.append(pltpu.make_async_copy(kr_hbm.at[layer, pg], buf_k.at[slot, rows], sem.at[1, slot]))
    return copies


def _sattn_kernel(pt_ref, qlat_ref, qrope_ref, ckvn_ref, krn_ref, ckv_hbm, kr_hbm, o_ref,
                  buf_c, buf_k, sem, *, layer, n_pages):
    b = pl.program_id(0)
    nb = pl.num_programs(0)
    slot = lax.rem(b, 2)
    copies = functools.partial(_page_copies, pt_ref, ckv_hbm, kr_hbm, buf_c, buf_k, sem, layer)

    @pl.when(b == 0)
    def _():
        for c in copies(0, 0, n_pages):
            c.start()

    @pl.when(b + 1 < nb)
    def _():
        for c in copies(b + 1, 1 - slot, n_pages):
            c.start()

    for c in copies(b, slot, n_pages):
        c.wait()

    round_bf = lambda a: a.astype(BF16).astype(F32)
    qlat = qlat_ref[0]
    qrope = qrope_ref[0]
    ckv_new = round_bf(ckvn_ref[0])
    kr_new = round_bf(krn_ref[0])
    ckv = buf_c[slot].astype(BF16)
    kr = buf_k[slot].astype(BF16)
    s = _dot_nt(qlat.astype(BF16), ckv) + _dot_nt(qrope.astype(BF16), kr)
    s_new = (jnp.sum(round_bf(qlat) * ckv_new, axis=-1, keepdims=True)
             + jnp.sum(round_bf(qrope) * kr_new, axis=-1, keepdims=True))
    m = jnp.maximum(jnp.max(s, axis=-1, keepdims=True), s_new)
    p = jnp.exp(s - m)
    p_new = jnp.exp(s_new - m)
    l = jnp.sum(p, axis=-1, keepdims=True) + p_new
    o = _dot(p.astype(BF16), ckv) + round_bf(p_new) * ckv_new
    o_ref[0] = o / l


def _sattn(page_table, qlat, qrope, ckv_new, kr_new, cache_ckv, cache_krope, layer):
    n_pages = page_table.shape[1]
    past = n_pages * PAGE_SIZE
    seq3 = lambda n, m: pl.BlockSpec((1, n, m), lambda b, pt: (b, 0, 0))
    return pl.pallas_call(
        functools.partial(_sattn_kernel, layer=layer, n_pages=n_pages),
        grid_spec=pltpu.PrefetchScalarGridSpec(
            num_scalar_prefetch=1,
            grid=(T_S,),
            in_specs=[seq3(MLA_HEADS, KV_RANK), seq3(MLA_HEADS, QK_ROPE), seq3(1, KV_RANK), seq3(1, QK_ROPE),
                      pl.BlockSpec(memory_space=pl.ANY), pl.BlockSpec(memory_space=pl.ANY)],
            out_specs=seq3(MLA_HEADS, KV_RANK),
            scratch_shapes=[pltpu.VMEM((2, past, KV_RANK), F32), pltpu.VMEM((2, past, QK_ROPE), F32),
                            pltpu.SemaphoreType.DMA((2, 2))],
        ),
        out_shape=jax.ShapeDtypeStruct((T_S, MLA_HEADS, KV_RANK), F32),
        compiler_params=_cparams("arbitrary"),
        name="sattn",
    )(page_table.reshape(-1), qlat.reshape(T_S, MLA_HEADS, KV_RANK), qrope.reshape(T_S, MLA_HEADS, QK_ROPE),
      ckv_new.reshape(T_S, 1, KV_RANK), kr_new.reshape(T_S, 1, QK_ROPE), cache_ckv, cache_krope)


def _merge_kernel(h_ref, ys_ref, o_ref, wps_ref, wpm_ref, wgs_ref, wgm_ref, wout_ref, g_ref, b_ref, *rest):
    out_ref = rest[-1]
    h = h_ref[...]
    hb = h.astype(BF16)
    y_s = _dot(ys_ref[...], wps_ref[...])
    y_m = _dot(o_ref[...], wpm_ref[...])
    merged = _sigmoid(_dot(hb, wgs_ref[...])) * y_s + _sigmoid(_dot(hb, wgm_ref[...])) * y_m
    f = _dot(merged.astype(BF16), wout_ref[...])
    out_ref[...] = _layer_norm(DN_ALPHA * h + f, g_ref[...], b_ref[...])


def _merge(hbuf, ys, o, lw, *, prompt, into=None):
    full = lambda shape: pl.BlockSpec(shape, lambda b, s: (0,) * len(shape))
    if prompt:
        tm, grid = TM, (BATCH, N_ST)
        row = lambda b, s: (b * N_ST + s, 0)
        ys_spec = pl.BlockSpec((tm, SSM_WIDTH), lambda b, s: (s, b))
        h_row = row
    else:
        tm, grid = T_S, (1, 1)
        row = lambda b, s: (0, 0)
        ys_spec = pl.BlockSpec((tm, SSM_WIDTH), row)
        h_row = lambda b, s: (S_BASE, 0)
    in_specs = [pl.BlockSpec((tm, D_MODEL), h_row), ys_spec, pl.BlockSpec((tm, MLA_HEADS * V_HEAD), row),
                full((SSM_WIDTH, D_MODEL)), full((MLA_HEADS * V_HEAD, D_MODEL)),
                full((D_MODEL, D_MODEL)), full((D_MODEL, D_MODEL)), full((D_MODEL, D_MODEL)),
                full((1, D_MODEL)), full((1, D_MODEL))]
    args = [hbuf, ys, o, lw["w_ps"], lw["w_pm"], lw["w_gs"], lw["w_gm"], lw["w_out"], lw["ln1_g"], lw["ln1_b"]]
    aliases = {}
    if into is not None:
        in_specs.append(pl.BlockSpec(memory_space=pl.ANY))
        args.append(into)
        aliases = {len(args) - 1: 0}
    return pl.pallas_call(
        _merge_kernel, grid=grid, in_specs=in_specs,
        out_specs=pl.BlockSpec((tm, D_MODEL), h_row),
        out_shape=jax.ShapeDtypeStruct((T_ALL, D_MODEL), F32),
        input_output_aliases=aliases,
        compiler_params=_cparams("parallel", "parallel"),
        name="merge_p" if prompt else "merge_s",
    )(*args)


def _router(h2, wr_ref, br_ref):
    logits = jnp.dot(h2, wr_ref[...], preferred_element_type=F32, precision=lax.Precision.HIGHEST) + br_ref[...]
    lane = lax.broadcasted_iota(I32, logits.shape, 1)
    big = jnp.int32(LANES)
    gmask = (lane >= N_EXPERTS) & (lane < N_EXPERTS + N_GROUPS)
    l1 = jnp.where(gmask, logits, -jnp.inf)
    e1 = jnp.exp(l1 - jnp.max(l1, axis=-1, keepdims=True))
    p1 = e1 / jnp.sum(e1, axis=-1, keepdims=True)
    top_p1 = jnp.max(p1, axis=-1, keepdims=True)
    g_idx = jnp.min(jnp.where(gmask & (p1 == top_p1), lane - N_EXPERTS, big), axis=-1, keepdims=True)
    lo = g_idx * EXPERTS_PER_GROUP
    emask = (lane >= lo) & (lane < lo + EXPERTS_PER_GROUP)
    l2 = jnp.where(emask, logits, -jnp.inf)
    e2 = jnp.exp(l2 - jnp.max(l2, axis=-1, keepdims=True))
    p2 = jnp.where(emask, e2 / jnp.sum(e2, axis=-1, keepdims=True), -1.0)
    pa = jnp.max(p2, axis=-1, keepdims=True)
    ia = jnp.min(jnp.where(p2 == pa, lane, big), axis=-1, keepdims=True)
    p2b = jnp.where(lane == ia, -1.0, p2)
    pb = jnp.max(p2b, axis=-1, keepdims=True)
    ib = jnp.min(jnp.where(p2b == pb, lane, big), axis=-1, keepdims=True)
    den = pa + pb
    wa = top_p1 * pa / den
    wb = top_p1 * pb / den
    out = jnp.where(lane == 0, ia.astype(F32), 0.0)
    out = jnp.where(lane == 1, ib.astype(F32), out)
    out = jnp.where(lane == 2, wa, out)
    out = jnp.where(lane == 3, wb, out)
    return out


def _mem_p_kernel(h_ref, k_ref, v_ref, wq_ref, wo_ref, g_ref, b_ref, wr_ref, br_ref, out_ref, route_ref):
    h = h_ref[...]
    q = _dot(h.astype(BF16), wq_ref[...])
    scale = 1.0 / math.sqrt(MEM_HEAD_DIM)
    outs = []
    for hd in range(MEM_HEADS):
        hs = slice(hd * MEM_HEAD_DIM, (hd + 1) * MEM_HEAD_DIM)
        s = _dot_nt(q[:, hs].astype(BF16), k_ref[:, hs].astype(BF16)) * scale
        e = jnp.exp(s - jnp.max(s, axis=-1, keepdims=True))
        p = e / jnp.sum(e, axis=-1, keepdims=True)
        outs.append(_dot(p.astype(BF16), v_ref[:, hs].astype(BF16)))
    o = jnp.concatenate(outs, axis=1)
    f = _dot(o.astype(BF16), wo_ref[...])
    h2 = _layer_norm(DN_ALPHA * h + f, g_ref[...], b_ref[...])
    out_ref[...] = h2
    route_ref[...] = _router(h2, wr_ref, br_ref)


def _mem_p(h1, mkv, lw):
    nq = MEM_HEADS * MEM_HEAD_DIM
    row = lambda b, s: (b * N_ST + s, 0)
    full = lambda shape: pl.BlockSpec(shape, lambda b, s: (0,) * len(shape))
    return pl.pallas_call(
        _mem_p_kernel, grid=(BATCH, N_ST),
        in_specs=[pl.BlockSpec((TM, D_MODEL), row),
                  pl.BlockSpec((MEM_TOKENS, nq), lambda b, s: (b, 0)),
                  pl.BlockSpec((MEM_TOKENS, nq), lambda b, s: (b, 1)),
                  full((D_MODEL, nq)), full((nq, D_MODEL)), full((1, D_MODEL)), full((1, D_MODEL)),
                  full((D_MODEL, LANES)), full((1, LANES))],
        out_specs=[pl.BlockSpec((TM, D_MODEL), row), pl.BlockSpec((TM, LANES), row)],
        out_shape=[jax.ShapeDtypeStruct((T_ALL, D_MODEL), F32), jax.ShapeDtypeStruct((T_ALL, LANES), F32)],
        compiler_params=_cparams("parallel", "parallel"),
        name="mem_p",
    )(h1, mkv, mkv, lw["w_mq"], lw["w_mo"], lw["ln2_g"], lw["ln2_b"], lw["w_r"], lw["b_r"])


def _mem_s_kernel(h_ref, k_ref, v_ref, wq_ref, wo_ref, g_ref, b_ref, wr_ref, br_ref, h2_in, route_in,
                  out_ref, route_ref):
    del h2_in, route_in
    round_bf = lambda a: a.astype(BF16).astype(F32)
    h = h_ref[...]
    q = round_bf(_dot(h.astype(BF16), wq_ref[...]))
    scale = 1.0 / math.sqrt(MEM_HEAD_DIM)
    rows = []
    for g in range(MEMS_G):
        prod = round_bf(k_ref[g]) * q[g:g + 1, :]
        vg = round_bf(v_ref[g])
        parts = []
        for hd in range(MEM_HEADS):
            hs = slice(hd * MEM_HEAD_DIM, (hd + 1) * MEM_HEAD_DIM)
            s = jnp.sum(prod[:, hs], axis=-1, keepdims=True) * scale
            e = jnp.exp(s - jnp.max(s, axis=0, keepdims=True))
            p = round_bf(e / jnp.sum(e, axis=0, keepdims=True))
            parts.append(jnp.sum(p * vg[:, hs], axis=0, keepdims=True))
        rows.append(jnp.concatenate(parts, axis=1))
    o = jnp.concatenate(rows, axis=0)
    f = _dot(o.astype(BF16), wo_ref[...])
    h2 = _layer_norm(DN_ALPHA * h + f, g_ref[...], b_ref[...])
    out_ref[...] = h2
    route_ref[...] = _router(h2, wr_ref, br_ref)


def _mem_s(h1, mem_k, mem_v, layer, lw, h2buf, routebuf):
    nq = MEM_HEADS * MEM_HEAD_DIM
    base = T_P // MEMS_G
    row = lambda i: (base + i, 0)
    full = lambda shape: pl.BlockSpec(shape, lambda i: (0,) * len(shape))
    kv_spec = pl.BlockSpec((None, MEMS_G, MEM_TOKENS, nq), lambda i: (layer, i, 0, 0))
    any_spec = pl.BlockSpec(memory_space=pl.ANY)
    return pl.pallas_call(
        _mem_s_kernel, grid=(T_S // MEMS_G,),
        in_specs=[pl.BlockSpec((MEMS_G, D_MODEL), row), kv_spec, kv_spec,
                  full((D_MODEL, nq)), full((nq, D_MODEL)), full((1, D_MODEL)), full((1, D_MODEL)),
                  full((D_MODEL, LANES)), full((1, LANES)), any_spec, any_spec],
        out_specs=[pl.BlockSpec((MEMS_G, D_MODEL), row), pl.BlockSpec((MEMS_G, LANES), row)],
        out_shape=[jax.ShapeDtypeStruct((T_ALL, D_MODEL), F32), jax.ShapeDtypeStruct((T_ALL, LANES), F32)],
        input_output_aliases={9: 0, 10: 1},
        compiler_params=_cparams("parallel"),
        name="mem_s",
    )(h1, mem_k, mem_v, lw["w_mq"], lw["w_mo"], lw["ln2_g"], lw["ln2_b"], lw["w_r"], lw["b_r"], h2buf, routebuf)


def _load_positions(pos_ref, idx, sem):
    cp = pltpu.make_async_copy(pos_ref.at[0, 0], idx, sem)
    cp.start()
    cp.wait()


def _dispatch_kernel(x_ref, pos_ref, xs_in, xs_ref, idx, sem_i, sem):
    del xs_in
    _load_positions(pos_ref, idx, sem_i)

    def row_copies(r):
        src = x_ref.at[pl.ds(r, 1)]
        return [pltpu.make_async_copy(src, xs_ref.at[pl.ds(idx[2 * r + k], 1)], sem) for k in range(2)]

    def issue(r, c):
        for cp in row_copies(r):
            cp.start()
        return c

    def drain(r, c):
        for cp in row_copies(r):
            cp.wait()
        return c

    lax.fori_loop(0, MOE_TT, issue, 0)
    lax.fori_loop(0, MOE_TT, drain, 0)


def _dispatch(h2, pos3, xs):
    return pl.pallas_call(
        _dispatch_kernel, grid=(N_TT,),
        in_specs=[pl.BlockSpec((MOE_TT, D_MODEL), lambda i: (i, 0)),
                  pl.BlockSpec((1, 1, POS_W), lambda i: (i, 0, 0)),
                  pl.BlockSpec(memory_space=pl.ANY)],
        out_specs=pl.BlockSpec(memory_space=pl.ANY),
        out_shape=jax.ShapeDtypeStruct((MOE_ROWS, D_MODEL), F32),
        scratch_shapes=[pltpu.SMEM((POS_W,), I32), pltpu.SemaphoreType.DMA(()), pltpu.SemaphoreType.DMA(())],
        input_output_aliases={2: 0},
        compiler_params=_cparams("arbitrary"),
        name="moe_dispatch",
    )(h2, pos3, xs)


def _gmm_kernel(te_ref, nv_ref, x_ref, wg_ref, wu_ref, wd_ref, o_ref):
    del te_ref

    @pl.when(pl.program_id(0) < nv_ref[0])
    def _():
        xb = x_ref[...].astype(BF16)
        g = _dot(xb, wg_ref[...].astype(BF16))
        u = _dot(xb, wu_ref[...].astype(BF16))
        hmid = g * _sigmoid(g) * u
        o_ref[...] = _dot(hmid.astype(BF16), wd_ref[...].astype(BF16))


def _gmm(tile_expert, n_valid, xs, w_eg, w_eu, w_ed, layer):
    tile = lambda i, te, nv: (jnp.minimum(i, nv[0] - 1), 0)
    return pl.pallas_call(
        _gmm_kernel,
        grid_spec=pltpu.PrefetchScalarGridSpec(
            num_scalar_prefetch=2, grid=(MOE_NT,),
            in_specs=[pl.BlockSpec((MOE_TM, D_MODEL), tile),
                      pl.BlockSpec((None, None, D_MODEL, D_EXPERT), lambda i, te, nv: (layer, te[i], 0, 0)),
                      pl.BlockSpec((None, None, D_MODEL, D_EXPERT), lambda i, te, nv: (layer, te[i], 0, 0)),
                      pl.BlockSpec((None, None, D_EXPERT, D_MODEL), lambda i, te, nv: (layer, te[i], 0, 0))],
            out_specs=pl.BlockSpec((MOE_TM, D_MODEL), tile),
        ),
        out_shape=jax.ShapeDtypeStruct((MOE_ROWS, D_MODEL), F32),
        compiler_params=_cparams("arbitrary"),
        name="moe_gmm",
    )(tile_expert, n_valid, xs, w_eg, w_eu, w_ed)


def _combine_kernel(h_ref, route_ref, pos_ref, g_ref, b_ref, eo_hbm, out_ref, idx, e0, e1, sem_i, sem):
    _load_positions(pos_ref, idx, sem_i)
    bufs = (e0, e1)

    def row_copies(r):
        return [pltpu.make_async_copy(eo_hbm.at[pl.ds(idx[2 * r + k], 1)], bufs[k].at[pl.ds(r, 1)], sem)
                for k in range(2)]

    def issue(r, c):
        for cp in row_copies(r):
            cp.start()
        return c

    def drain(r, c):
        for cp in row_copies(r):
            cp.wait()
        return c

    lax.fori_loop(0, MOE_TT, issue, 0)
    lax.fori_loop(0, MOE_TT, drain, 0)
    route = route_ref[...]
    y = route[:, 2:3] * e0[...] + route[:, 3:4] * e1[...]
    out_ref[...] = _layer_norm(DN_ALPHA * h_ref[...] + y, g_ref[...], b_ref[...])


def _combine(h2, route, pos3, eo, lw):
    row = lambda i: (i, 0)
    return pl.pallas_call(
        _combine_kernel, grid=(N_TT,),
        in_specs=[pl.BlockSpec((MOE_TT, D_MODEL), row), pl.BlockSpec((MOE_TT, LANES), row),
                  pl.BlockSpec((1, 1, POS_W), lambda i: (i, 0, 0)),
                  pl.BlockSpec((1, D_MODEL), lambda i: (0, 0)), pl.BlockSpec((1, D_MODEL), lambda i: (0, 0)),
                  pl.BlockSpec(memory_space=pl.ANY)],
        out_specs=pl.BlockSpec((MOE_TT, D_MODEL), row),
        out_shape=jax.ShapeDtypeStruct((T_ALL, D_MODEL), F32),
        scratch_shapes=[pltpu.SMEM((POS_W,), I32), pltpu.VMEM((MOE_TT, D_MODEL), F32),
                        pltpu.VMEM((MOE_TT, D_MODEL), F32),
                        pltpu.SemaphoreType.DMA(()), pltpu.SemaphoreType.DMA(())],
        compiler_params=_cparams("arbitrary"),
        name="moe_combine",
    )(h2, route, pos3, lw["ln3_g"], lw["ln3_b"], eo)


def _route_plan(route):
    e = route[:, :2].astype(I32).reshape(-1)
    onehot = (e[:, None] == jnp.arange(N_EXPERTS, dtype=I32)[None, :]).astype(I32)
    csum = jnp.cumsum(onehot, axis=0)
    rank = jnp.sum(csum * onehot, axis=1) - 1
    counts = csum[-1]
    tiles = (counts + MOE_TM - 1) // MOE_TM
    tile_end = jnp.cumsum(tiles)
    row_off = (tile_end - tiles) * MOE_TM
    pos = jnp.sum(onehot * row_off[None, :], axis=1) + rank
    n_valid = tile_end[-1]
    tid = jnp.minimum(jnp.arange(MOE_NT, dtype=I32), n_valid - 1)
    tile_expert = jnp.sum((tid[:, None] >= tile_end[None, :]).astype(I32), axis=1)
    pos3 = jnp.pad(pos.reshape(N_TT, 2 * MOE_TT), ((0, 0), (0, POS_W - 2 * MOE_TT))).reshape(N_TT, 1, POS_W)
    return pos3.astype(I32), tile_expert.astype(I32), n_valid.reshape(1).astype(I32)


def _rot_cols(w):
    half = QK_ROPE // 2
    return jnp.concatenate([-w[..., half:], w[..., :half]], axis=-1)


def _pad_rope_cols(w):
    pad = [(0, 0)] * (w.ndim - 1) + [(QK_NOPE, HEAD_PAD - QK_NOPE - QK_ROPE)]
    return jnp.pad(w, pad)


def _block_diag(blocks):
    n = blocks.shape[-3]
    eye = jnp.eye(n, dtype=blocks.dtype)
    out = blocks[..., :, :, None, :] * eye[:, None, :, None]
    return out.reshape(blocks.shape[:-3] + (n * blocks.shape[-2], n * blocks.shape[-1]))


def _prepare_weights(p, disc):
    a_re, a_im, bb_re, bb_im = disc
    w_in = p["w_in"]
    w_kr = w_in[:, :, O_KR:O_GS]
    lw = {}
    lw["wmain"] = jnp.concatenate(
        [w_in[:, :, :O_KR], _pad_rope_cols(w_kr), _pad_rope_cols(_rot_cols(w_kr))], axis=2).astype(BF16)
    lw["w_gs"] = w_in[:, :, O_GS:O_GM].astype(BF16)
    lw["w_gm"] = w_in[:, :, O_GM:].astype(BF16)
    w_uq = p["w_uq"]
    uq_n, uq_r = w_uq[..., :QK_NOPE], w_uq[..., QK_NOPE:]
    zeros = lambda n: jnp.zeros(w_uq.shape[:3] + (n,), F32)
    lw["wa"] = jnp.concatenate([uq_n, uq_r, zeros(HEAD_PAD - QK_NOPE - QK_ROPE)], axis=-1).reshape(
        DEPTH, Q_RANK, QK_W).astype(BF16)
    lw["wb"] = jnp.concatenate([zeros(QK_NOPE), _rot_cols(uq_r), zeros(HEAD_PAD - QK_NOPE - QK_ROPE)],
                               axis=-1).reshape(DEPTH, Q_RANK, QK_W).astype(BF16)
    lw["wa_r"] = uq_r.reshape(DEPTH, Q_RANK, MLA_HEADS * QK_ROPE).astype(BF16)
    lw["wb_r"] = _rot_cols(uq_r).reshape(DEPTH, Q_RANK, MLA_HEADS * QK_ROPE).astype(BF16)
    w_uk, w_uv = p["w_uk"], p["w_uv"]
    lw["wk_pad"] = jnp.pad(w_uk, ((0, 0), (0, 0), (0, 0), (0, HEAD_PAD - QK_NOPE))).reshape(
        DEPTH, KV_RANK, QK_W).astype(BF16)
    v_pair = w_uv.reshape(DEPTH, KV_RANK, MLA_HEADS // 2, 2, V_HEAD)
    v_even = jnp.pad(v_pair[:, :, :, 0], ((0, 0), (0, 0), (0, 0), (0, HEAD_PAD - V_HEAD)))
    v_odd = jnp.pad(v_pair[:, :, :, 1], ((0, 0), (0, 0), (0, 0), (HEAD_PAD - V_HEAD, 0)))
    lw["wv_pad"] = jnp.stack([v_even, v_odd], axis=3).reshape(DEPTH, KV_RANK, QK_W).astype(BF16)
    lw["wukt_pad"] = jnp.pad(jnp.transpose(w_uk, (0, 2, 3, 1)),
                             ((0, 0), (0, 0), (0, HEAD_PAD - QK_NOPE), (0, 0))).astype(BF16)
    lw["w_uv_bd"] = _block_diag(jnp.transpose(w_uv, (0, 2, 1, 3))).astype(BF16)
    lw["q_norm_g"] = p["q_norm_g"].reshape(DEPTH, 1, Q_RANK)
    lw["kv_norm_g"] = p["kv_norm_g"].reshape(DEPTH, 1, KV_RANK)
    for name in ("w_pm", "w_glu", "w_ps", "w_out"):
        lw[name] = p[name].astype(BF16)
    nq = MEM_HEADS * MEM_HEAD_DIM
    lw["w_mq"] = p["w_mq"].reshape(DEPTH, D_MODEL, nq).astype(BF16)
    lw["w_mo"] = p["w_mo"].reshape(DEPTH, nq, D_MODEL).astype(BF16)
    lw["w_mkv"] = jnp.concatenate([p["w_mk"].reshape(DEPTH, D_MODEL, nq), p["w_mv"].reshape(DEPTH, D_MODEL, nq)],
                                  axis=2).astype(BF16)
    for name in ("ln1_g", "ln1_b", "ln2_g", "ln2_b", "ln3_g", "ln3_b"):
        lw[name] = p[name].reshape(DEPTH, 1, D_MODEL)
    lw["w_r"] = jnp.pad(jnp.concatenate([p["w_r2"], p["w_r1"]], axis=2),
                        ((0, 0), (0, 0), (0, LANES - N_EXPERTS - N_GROUPS)))
    lw["b_r"] = jnp.pad(jnp.concatenate([p["b_r2"], p["b_r1"]], axis=1),
                        ((0, 0), (0, LANES - N_EXPERTS - N_GROUPS))).reshape(DEPTH, 1, LANES)
    n_blk = SSM_WIDTH // LANES
    gpb = SSM_GROUPS // n_blk
    blk = lambda a: a.reshape(DEPTH, n_blk, gpb, SSM_STATE, SSM_GROUP)
    to_b = lambda a: _block_diag(jnp.transpose(blk(a), (0, 1, 2, 4, 3)))
    lw["bm"] = jnp.concatenate([to_b(bb_re), to_b(bb_im)], axis=-1).astype(BF16)
    cblk = lambda a: a.reshape(DEPTH, n_blk, gpb, SSM_GROUP, SSM_STATE)
    to_c = lambda a: _block_diag(jnp.transpose(cblk(a), (0, 1, 2, 4, 3)))
    lw["cm"] = jnp.concatenate([to_c(p["c_re"]), -to_c(p["c_im"])], axis=-2).astype(BF16)
    lw["a_re"] = a_re.reshape(DEPTH, 1, SSM_LANES)
    lw["a_im"] = a_im.reshape(DEPTH, 1, SSM_LANES)
    lw["d_skip"] = p["d_skip"].reshape(DEPTH, 1, SSM_WIDTH)
    return lw


def _rope_tables(pos, n_rows):
    inv = 1.0 / (ROPE_THETA ** (jnp.arange(0, QK_ROPE, 2, dtype=F32) / QK_ROPE))
    ang = pos.astype(F32)[:, None] * inv[None, :]
    cos2 = jnp.concatenate([jnp.cos(ang)] * 2, axis=1)
    sin2 = jnp.concatenate([jnp.sin(ang)] * 2, axis=1)
    cos2 = jnp.broadcast_to(cos2, (n_rows, QK_ROPE))
    sin2 = jnp.broadcast_to(sin2, (n_rows, QK_ROPE))
    scale = 1.0 / math.sqrt(QK_NOPE + QK_ROPE)
    z = lambda n: jnp.zeros((n_rows, n), F32)
    tail = HEAD_PAD - QK_NOPE - QK_ROPE
    return {
        "caq": scale * jnp.concatenate([jnp.ones((n_rows, QK_NOPE), F32), cos2, z(tail)], axis=1),
        "sbq": scale * jnp.concatenate([z(QK_NOPE), sin2, z(tail)], axis=1),
        "ck": jnp.concatenate([z(QK_NOPE), cos2, z(tail)], axis=1),
        "sk": jnp.concatenate([z(QK_NOPE), sin2, z(tail)], axis=1),
        "cr": scale * jnp.concatenate([cos2] * MLA_HEADS, axis=1),
        "sr": scale * jnp.concatenate([sin2] * MLA_HEADS, axis=1),
    }


def kernel(x_prompt, x_sample, cache_ckv, cache_krope, cache_mem_k, cache_mem_v, state_ssm_re, state_ssm_im, page_table, mem_prompt, mem_ln_g, mem_ln_b, w_in, w_uq, q_norm_g, kv_norm_g, w_uk, w_uv, w_pm, lam_re, lam_im, log_dt, b_re, b_im, c_re, c_im, d_skip, w_glu, w_ps, w_out, ln1_g, ln1_b, w_mq, w_mk, w_mv, w_mo, ln2_g, ln2_b, w_r1, b_r1, w_r2, b_r2, w_eg, w_eu, w_ed, ln3_g, ln3_b):
    assert x_prompt.shape == (BATCH, SEQ, D_MODEL) and x_sample.shape == (DEC_BATCH, 1, D_MODEL)
    params = dict(w_in=w_in, w_uq=w_uq, q_norm_g=q_norm_g, kv_norm_g=kv_norm_g, w_uk=w_uk, w_uv=w_uv, w_pm=w_pm,
                  c_re=c_re, c_im=c_im, d_skip=d_skip, w_glu=w_glu, w_ps=w_ps, w_out=w_out,
                  ln1_g=ln1_g, ln1_b=ln1_b, w_mq=w_mq, w_mk=w_mk, w_mv=w_mv, w_mo=w_mo, ln2_g=ln2_g, ln2_b=ln2_b,
                  w_r1=w_r1, b_r1=b_r1, w_r2=w_r2, b_r2=b_r2, ln3_g=ln3_g, ln3_b=ln3_b)
    disc = _discretize(lam_re, lam_im, log_dt, b_re, b_im)
    weights = _prepare_weights(params, disc)
    n_pages = page_table.shape[1]
    tabs_p = _rope_tables(jnp.arange(SEQ), SEQ)
    tabs_s = _rope_tables(n_pages * PAGE_SIZE + jnp.arange(1), T_S)
    nq = MEM_HEADS * MEM_HEAD_DIM

    m_p = _ln_rows(mem_prompt.reshape(BATCH * MEM_TOKENS, D_MODEL), mem_ln_g, mem_ln_b, MEM_TOKENS)
    mem_k = cache_mem_k.reshape(DEPTH, DEC_BATCH, MEM_TOKENS, nq)
    mem_v = cache_mem_v.reshape(DEPTH, DEC_BATCH, MEM_TOKENS, nq)
    zeros_state = jnp.zeros((BATCH, SSM_LANES), F32)

    hbuf = jnp.concatenate([x_prompt.reshape(T_P, D_MODEL), x_sample.reshape(T_S, D_MODEL)], axis=0)
    xs = jnp.zeros((MOE_ROWS, D_MODEL), F32)
    outs = {k: [] for k in ("p_sre", "p_sim", "p_ckv", "p_kr", "p_mk", "p_mv", "s_sre", "s_sim", "s_ckv", "s_kr")}
    for l in range(DEPTH):
        lw = {k: v[l] for k, v in weights.items()}
        u_p, q_p, ckv_p, kr_p, k_p, v_p = _proj_in(hbuf, lw, tabs_p, prompt=True)
        u_s, ckv_s, kr_s, qlat_s, qrope_s = _proj_in(hbuf, lw, tabs_s, prompt=False)
        ys_p, sre_p, sim_p = _s5(u_p.reshape(SEQ * BATCH, SSM_WIDTH), zeros_state, zeros_state, lw,
                                 n_rows=BATCH, n_steps=S5_LT)
        ys_s, sre_s, sim_s = _s5(u_s, state_ssm_re[l].reshape(T_S, SSM_LANES),
                                 state_ssm_im[l].reshape(T_S, SSM_LANES), lw, n_rows=T_S, n_steps=1)
        o_p = _flash(q_p, k_p, v_p)
        kr_s32 = kr_s[:, QK_NOPE:QK_NOPE + QK_ROPE]
        olat_s = _sattn(page_table, qlat_s, qrope_s, ckv_s, kr_s32, cache_ckv, cache_krope, l)
        o_s = _mm(olat_s.reshape(T_S, MLA_HEADS * KV_RANK), lw["w_uv_bd"], T_S, BF16)
        h1 = _merge(hbuf, ys_p.reshape(SEQ, BATCH * SSM_WIDTH), o_p, lw, prompt=True)
        h1 = _merge(hbuf, ys_s, o_s, lw, prompt=False, into=h1)
        mkv = _mm(m_p, lw["w_mkv"], MEM_TOKENS)
        h2, route = _mem_p(h1, mkv, lw)
        h2, route = _mem_s(h1, mem_k, mem_v, l, lw, h2, route)
        pos3, tile_expert, n_valid = _route_plan(route)
        xs = _dispatch(h2, pos3, xs)
        eo = _gmm(tile_expert, n_valid, xs, w_eg, w_eu, w_ed, l)
        hbuf = _combine(h2, route, pos3, eo, lw)
        outs["p_sre"].append(sre_p.reshape(BATCH, SSM_GROUPS, SSM_STATE))
        outs["p_sim"].append(sim_p.reshape(BATCH, SSM_GROUPS, SSM_STATE))
        outs["p_ckv"].append(ckv_p.reshape(BATCH, SEQ, KV_RANK))
        outs["p_kr"].append(kr_p[:, QK_NOPE:QK_NOPE + QK_ROPE].reshape(BATCH, SEQ, QK_ROPE))
        outs["p_mk"].append(mkv[:, :nq].reshape(BATCH, MEM_TOKENS, MEM_HEADS, MEM_HEAD_DIM))
        outs["p_mv"].append(mkv[:, nq:].reshape(BATCH, MEM_TOKENS, MEM_HEADS, MEM_HEAD_DIM))
        outs["s_sre"].append(sre_s.reshape(DEC_BATCH, SSM_GROUPS, SSM_STATE))
        outs["s_sim"].append(sim_s.reshape(DEC_BATCH, SSM_GROUPS, SSM_STATE))
        outs["s_ckv"].append(ckv_s.reshape(DEC_BATCH, 1, KV_RANK))
        outs["s_kr"].append(kr_s32.reshape(DEC_BATCH, 1, QK_ROPE))

    st = {k: jnp.stack(v) for k, v in outs.items()}
    return (hbuf[:T_P].reshape(BATCH, SEQ, D_MODEL), hbuf[T_P:].reshape(DEC_BATCH, 1, D_MODEL),
            st["p_sre"], st["p_sim"], st["p_ckv"], st["p_kr"], st["p_mk"], st["p_mv"],
            st["s_sre"], st["s_sim"], st["s_ckv"], st["s_kr"])
```
